```python
import math, functools
import jax, jax.numpy as jnp
from jax import lax
import numpy as np

D_MODEL = 1024
BATCH = 16
SEQ = 2048
DEPTH = 1
DEC_BATCH = 128
DEC_SEQ = 1
PAST_LEN = 8192
PAGE_SIZE = 128

D_MIX = D_MODEL
GLA_WIDTH = D_MIX // 2
MLA_WIDTH = D_MIX - GLA_WIDTH
GLA_HEADS = 4
GLA_DV = GLA_WIDTH // GLA_HEADS
GLA_DK = GLA_DV // 2
GLA_GATE_RANK = 16
GLA_TAU = 16.0
GLA_CHUNK = 64
MLA_HEADS = 8
MLA_DV = MLA_WIDTH // MLA_HEADS
MLA_D_NOPE = 64
MLA_D_ROPE = 32
MLA_Q_RANK = 384
MLA_KV_RANK = 256
MLA_SCALE = (MLA_D_NOPE + MLA_D_ROPE) ** -0.5
ROPE_THETA = 10000.0
Q_BLOCK = 128
MEM_TOKENS = 256
X_HEADS = 4
X_DH = D_MODEL // X_HEADS
N_GROUPS = 4
EXPERTS_PER_GROUP = 8
N_EXPERTS = N_GROUPS * EXPERTS_PER_GROUP
TOP_K_IN_GROUP = 2
D_EXPERT = 256
ALPHA = (2 * DEPTH) ** 0.25
BETA = (8 * DEPTH) ** -0.25
LN_EPS = 1e-5
RMS_EPS = 1e-6
IN_SIZES = (GLA_HEADS * GLA_DK, GLA_HEADS * GLA_DK, GLA_WIDTH, GLA_GATE_RANK, GLA_WIDTH,
            MLA_Q_RANK, MLA_KV_RANK, MLA_D_ROPE)
D_IN_PROJ = sum(IN_SIZES)

kernel_name = 'hymba_gla_mla_hmoe_deepnorm_step'


def layer_norm(x, g, b):
    xf = x.astype(jnp.float32)
    mu = jnp.mean(xf, axis=-1, keepdims=True)
    var = jnp.mean(jnp.square(xf - mu), axis=-1, keepdims=True)
    return ((xf - mu) * lax.rsqrt(var + LN_EPS) * g + b).astype(x.dtype)


def rms_norm(x, g):
    xf = x.astype(jnp.float32)
    return (xf * lax.rsqrt(jnp.mean(xf * xf, axis=-1, keepdims=True) + RMS_EPS) * g).astype(x.dtype)


def rope_angles(positions):
    inv = ROPE_THETA ** (-jnp.arange(0, MLA_D_ROPE, 2, dtype=jnp.float32) / MLA_D_ROPE)
    ang = positions.astype(jnp.float32)[:, None] * inv[None, :]
    return jnp.cos(ang), jnp.sin(ang)


def apply_rope(x, cos, sin):
    x1, x2 = jnp.split(x.astype(jnp.float32), 2, axis=-1)
    c, s = cos[None, :, None, :], sin[None, :, None, :]
    return jnp.concatenate([x1 * c - x2 * s, x1 * s + x2 * c], axis=-1).astype(x.dtype)


def gla_recurrence(q, k, v, log_a, s0, chunk):
    B, L, H, dk = q.shape
    dv = v.shape[-1]
    n = L // chunk
    def to_chunks(t):
        return t.astype(jnp.float32).reshape(B, n, chunk, H, t.shape[-1]).transpose(1, 0, 3, 2, 4)
    causal = jnp.tril(jnp.ones((chunk, chunk), bool))
    def step(s, inp):
        qi, ki, vi, gi = inp
        b = jnp.cumsum(gi, axis=2)
        b_last = b[:, :, -1:, :]
        rel = jnp.where(causal[None, None, :, :, None], b[:, :, :, None, :] - b[:, :, None, :, :], -jnp.inf)
        attn = jnp.einsum('bhid,bhjd,bhijd->bhij', qi, ki, jnp.exp(rel))
        o = jnp.einsum('bhij,bhjv->bhiv', attn, vi) + jnp.einsum('bhid,bhdv->bhiv', qi * jnp.exp(b), s)
        s_new = jnp.exp(b_last)[:, :, 0, :, None] * s + jnp.einsum('bhjd,bhjv->bhdv', ki * jnp.exp(b_last - b), vi)
        return s_new, o
    s_fin, o = lax.scan(step, s0.astype(jnp.float32), (to_chunks(q), to_chunks(k), to_chunks(v), to_chunks(log_a)))
    o = o.transpose(1, 0, 3, 2, 4).reshape(B, L, H, dv)
    return o.astype(v.dtype), s_fin.astype(s0.dtype)


def gla_group(q_in, k_in, v_in, a_in, r_in, s0, w_gla_gate, b_gla_gate, gla_norm_g):
    B, L, _ = q_in.shape
    def heads(t, d):
        return t.reshape(B, L, GLA_HEADS, d)
    q = heads(q_in, GLA_DK) * (GLA_DK ** -0.5)
    k = heads(k_in, GLA_DK)
    v = heads(v_in, GLA_DV)
    log_a = jax.nn.log_sigmoid((a_in @ w_gla_gate + b_gla_gate).astype(jnp.float32)) / GLA_TAU
    o, s = gla_recurrence(q, k, v, heads(log_a, GLA_DK), s0, math.gcd(L, GLA_CHUNK))
    o = rms_norm(o, gla_norm_g) * jax.nn.silu(heads(r_in, GLA_DV))
    return o.reshape(B, L, GLA_WIDTH), s


def mla_project(cq_in, ckv_in, kr_in, positions, mla_q_norm_g, w_uq, mla_kv_norm_g):
    q = jnp.einsum('blr,rhd->blhd', rms_norm(cq_in, mla_q_norm_g), w_uq)
    cos, sin = rope_angles(positions)
    q_nope = q[..., :MLA_D_NOPE]
    q_pe = apply_rope(q[..., MLA_D_NOPE:], cos, sin)
    c_kv = rms_norm(ckv_in, mla_kv_norm_g)
    k_pe = apply_rope(kr_in[:, :, None, :], cos, sin)[:, :, 0, :]
    return q_nope, q_pe, c_kv, k_pe


def mla_prompt_attention(q_nope, q_pe, c_kv, k_pe, w_uk, w_uv):
    B, L, H, _ = q_nope.shape
    k_nope = jnp.einsum('blr,rhd->blhd', c_kv, w_uk)
    v = jnp.einsum('blr,rhd->blhd', c_kv, w_uv)
    n_blk = L // Q_BLOCK
    def blocks(t):
        return t.reshape(B, n_blk, Q_BLOCK, *t.shape[2:]).swapaxes(0, 1)
    key_pos = jnp.arange(L)
    def one_block(args):
        qn, qp, blk = args
        s = (jnp.einsum('bqhd,bkhd->bhqk', qn, k_nope, preferred_element_type=jnp.float32)
             + jnp.einsum('bqhd,bkd->bhqk', qp, k_pe, preferred_element_type=jnp.float32)) * MLA_SCALE
        q_pos = blk * Q_BLOCK + jnp.arange(Q_BLOCK)
        s = jnp.where(q_pos[:, None] >= key_pos[None, :], s, -jnp.inf)
        p = jax.nn.softmax(s, axis=-1).astype(v.dtype)
        return jnp.einsum('bhqk,bkhd->bqhd', p, v)
    o = lax.map(one_block, (blocks(q_nope), blocks(q_pe), jnp.arange(n_blk)))
    return o.swapaxes(0, 1).reshape(B, L, H * MLA_DV)


def mla_sample_attention(q_nope, q_pe, c_new, kpe_new, w_uk, w_uv, past_c, past_kpe):
    B, T, H, _ = q_nope.shape
    f32 = jnp.float32
    q_lat = jnp.einsum('bthd,rhd->bthr', q_nope, w_uk)
    s_past = (jnp.einsum('bthr,bpr->bhtp', q_lat, past_c, preferred_element_type=f32)
              + jnp.einsum('bthd,bpd->bhtp', q_pe, past_kpe, preferred_element_type=f32))
    s_new = (jnp.einsum('bthr,bur->bhtu', q_lat, c_new, preferred_element_type=f32)
             + jnp.einsum('bthd,bud->bhtu', q_pe, kpe_new, preferred_element_type=f32))
    s_new = jnp.where(jnp.tril(jnp.ones((T, T), bool)), s_new, -jnp.inf)
    p = jax.nn.softmax(jnp.concatenate([s_past, s_new], axis=-1) * MLA_SCALE, axis=-1).astype(past_c.dtype)
    n_past = past_c.shape[1]
    o_lat = (jnp.einsum('bhtp,bpr->bthr', p[..., :n_past], past_c)
             + jnp.einsum('bhtu,bur->bthr', p[..., n_past:], c_new))
    o = jnp.einsum('bthr,rhd->bthd', o_lat, w_uv)
    return o.reshape(B, T, H * MLA_DV)


def mixer(h, positions, gla_s0, mla_attend, w_in, w_gla_gate, b_gla_gate, gla_norm_g,
          mla_q_norm_g, w_uq, mla_kv_norm_g, w_uk, w_uv, w_out):
    z = h @ w_in
    split_pts = np.cumsum(IN_SIZES)[:-1].tolist()
    q_g, k_g, v_g, a_g, r_g, cq, ckv, kr = jnp.split(z, split_pts, axis=-1)
    gla_o, gla_s = gla_group(q_g, k_g, v_g, a_g, r_g, gla_s0, w_gla_gate, b_gla_gate, gla_norm_g)
    q_nope, q_pe, c_kv, k_pe = mla_project(cq, ckv, kr, positions, mla_q_norm_g, w_uq, mla_kv_norm_g)
    mla_o = mla_attend(q_nope, q_pe, c_kv, k_pe, w_uk, w_uv)
    out = jnp.concatenate([gla_o, mla_o], axis=-1) @ w_out
    return out, gla_s, c_kv, k_pe


def memory_kv(mem, w_mk, w_mv):
    B, M, _ = mem.shape
    return (mem @ w_mk).reshape(B, M, X_HEADS, X_DH), (mem @ w_mv).reshape(B, M, X_HEADS, X_DH)


def cross_attention(h, mem_k, mem_v, w_xq, w_xo):
    B, L, _ = h.shape
    q = (h @ w_xq).reshape(B, L, X_HEADS, X_DH)
    s = jnp.einsum('blhd,bmhd->bhlm', q, mem_k, preferred_element_type=jnp.float32) * (X_DH ** -0.5)
    p = jax.nn.softmax(s, axis=-1).astype(mem_v.dtype)
    o = jnp.einsum('bhlm,bmhd->blhd', p, mem_v).reshape(B, L, X_HEADS * X_DH)
    return o @ w_xo


def hier_moe(h, w_grp, b_grp, w_rtr, b_rtr, w_e_gate, w_e_up, w_e_down):
    B, L, D = h.shape
    t = h.reshape(B * L, D)
    T = t.shape[0]
    grp_logits = jnp.einsum('td,dg->tg', t, w_grp, preferred_element_type=jnp.float32)
    grp_prob = jax.nn.softmax(grp_logits, axis=-1)
    g_idx = jnp.argmax(grp_logits + b_grp, axis=-1)
    p_g = jnp.take_along_axis(grp_prob, g_idx[:, None], axis=1)
    e_logits = jnp.einsum('td,de->te', t, w_rtr, preferred_element_type=jnp.float32)
    e_logits = e_logits.reshape(T, N_GROUPS, EXPERTS_PER_GROUP)
    in_grp = jnp.take_along_axis(e_logits, g_idx[:, None, None], axis=1)[:, 0]
    in_bias = b_rtr.reshape(N_GROUPS, EXPERTS_PER_GROUP)[g_idx]
    _, top_i = lax.top_k(in_grp + in_bias, TOP_K_IN_GROUP)
    w_sel = jax.nn.softmax(jnp.take_along_axis(in_grp, top_i, axis=1), axis=-1)
    expert_ids = g_idx[:, None] * EXPERTS_PER_GROUP + top_i
    gates = jnp.sum(jax.nn.one_hot(expert_ids, N_EXPERTS, dtype=jnp.float32)
                    * (p_g * w_sel)[..., None], axis=1)
    y = jnp.zeros((T, D), jnp.float32)
    for e in range(N_EXPERTS):
        he = jax.nn.silu(t @ w_e_gate[e]) * (t @ w_e_up[e])
        y = y + gates[:, e:e + 1] * (he @ w_e_down[e])
    return y.astype(h.dtype).reshape(B, L, D)


def setup_inputs(seed: int = 0) -> dict:
    key = jax.random.key(seed)
    keys = jax.random.split(key, 48)
    idx = iter(range(48))
    def nrm(shape, scale):
        return jax.random.normal(keys[next(idx)], shape, jnp.float32) * scale
    def gain(shape):
        return 1.0 + nrm(shape, 0.02)
    n_pages = PAST_LEN // PAGE_SIZE
    used = DEC_BATCH * n_pages
    n_phys = used + used // 4
    Ld = DEPTH
    return {
        'x_prompt': nrm((BATCH, SEQ, D_MODEL), 1.0),
        'x_sample': nrm((DEC_BATCH, DEC_SEQ, D_MODEL), 1.0),
        'mem_prompt': nrm((BATCH, MEM_TOKENS, D_MODEL), 1.0),
        'state_gla': nrm((Ld, DEC_BATCH, GLA_HEADS, GLA_DK, GLA_DV), 1.0),
        'cache_latent': nrm((Ld, n_phys, PAGE_SIZE, MLA_KV_RANK), 1.0),
        'cache_krope': nrm((Ld, n_phys, PAGE_SIZE, MLA_D_ROPE), 1.0),
        'cache_mem_k': nrm((Ld, DEC_BATCH, MEM_TOKENS, X_HEADS, X_DH), 1.0),
        'cache_mem_v': nrm((Ld, DEC_BATCH, MEM_TOKENS, X_HEADS, X_DH), 1.0),
        'page_table': jax.random.permutation(keys[next(idx)], n_phys)[:used].reshape(DEC_BATCH, n_pages).astype(jnp.int32),
        'w_in': nrm((Ld, D_MODEL, D_IN_PROJ), D_MODEL ** -0.5),
        'w_gla_gate': nrm((Ld, GLA_GATE_RANK, GLA_HEADS * GLA_DK), GLA_GATE_RANK ** -0.5),
        'b_gla_gate': nrm((Ld, GLA_HEADS * GLA_DK), 0.1),
        'gla_norm_g': gain((Ld, GLA_DV)),
        'mla_q_norm_g': gain((Ld, MLA_Q_RANK)),
        'w_uq': nrm((Ld, MLA_Q_RANK, MLA_HEADS, MLA_D_NOPE + MLA_D_ROPE), MLA_Q_RANK ** -0.5),
        'mla_kv_norm_g': gain((Ld, MLA_KV_RANK)),
        'w_uk': nrm((Ld, MLA_KV_RANK, MLA_HEADS, MLA_D_NOPE), MLA_KV_RANK ** -0.5),
        'w_uv': nrm((Ld, MLA_KV_RANK, MLA_HEADS, MLA_DV), MLA_KV_RANK ** -0.5),
        'w_out': nrm((Ld, D_MIX, D_MODEL), BETA * D_MIX ** -0.5),
        'ln1_g': gain((Ld, D_MODEL)),
        'ln1_b': nrm((Ld, D_MODEL), 0.02),
        'w_xq': nrm((Ld, D_MODEL, X_HEADS * X_DH), D_MODEL ** -0.5),
        'w_mk': nrm((Ld, D_MODEL, X_HEADS * X_DH), D_MODEL ** -0.5),
        'w_mv': nrm((Ld, D_MODEL, X_HEADS * X_DH), D_MODEL ** -0.5),
        'w_xo': nrm((Ld, X_HEADS * X_DH, D_MODEL), BETA * (X_HEADS * X_DH) ** -0.5),
        'ln2_g': gain((Ld, D_MODEL)),
        'ln2_b': nrm((Ld, D_MODEL), 0.02),
        'w_grp': nrm((Ld, D_MODEL, N_GROUPS), D_MODEL ** -0.5),
        'b_grp': nrm((Ld, N_GROUPS), 0.01),
        'w_rtr': nrm((Ld, D_MODEL, N_EXPERTS), D_MODEL ** -0.5),
        'b_rtr': nrm((Ld, N_EXPERTS), 0.01),
        'w_e_gate': nrm((Ld, N_EXPERTS, D_MODEL, D_EXPERT), D_MODEL ** -0.5),
        'w_e_up': nrm((Ld, N_EXPERTS, D_MODEL, D_EXPERT), D_MODEL ** -0.5),
        'w_e_down': nrm((Ld, N_EXPERTS, D_EXPERT, D_MODEL), BETA * D_EXPERT ** -0.5),
        'ln3_g': gain((Ld, D_MODEL)),
        'ln3_b': nrm((Ld, D_MODEL), 0.02),
    }


def reference(x_prompt, x_sample, mem_prompt, state_gla, cache_latent, cache_krope, cache_mem_k,
              cache_mem_v, page_table, w_in, w_gla_gate, b_gla_gate, gla_norm_g, mla_q_norm_g, w_uq,
              mla_kv_norm_g, w_uk, w_uv, w_out, ln1_g, ln1_b, w_xq, w_mk, w_mv, w_xo, ln2_g, ln2_b,
              w_grp, b_grp, w_rtr, b_rtr, w_e_gate, w_e_up, w_e_down, ln3_g, ln3_b):
    Bp, Lp, _ = x_prompt.shape
    Bd, Ld, _ = x_sample.shape
    past_len = page_table.shape[1] * cache_latent.shape[2]
    pos_p = jnp.arange(Lp)
    pos_d = past_len + jnp.arange(Ld)
    hp, hd = x_prompt, x_sample
    sp_list, cp_list, kp_list, mk_list, mv_list = [], [], [], [], []
    sd_list, cd_list, kd_list = [], [], []
    for l in range(DEPTH):
        mix_w = (w_in[l], w_gla_gate[l], b_gla_gate[l], gla_norm_g[l], mla_q_norm_g[l], w_uq[l],
                 mla_kv_norm_g[l], w_uk[l], w_uv[l], w_out[l])
        moe_w = (w_grp[l], b_grp[l], w_rtr[l], b_rtr[l], w_e_gate[l], w_e_up[l], w_e_down[l])
        s0 = jnp.zeros((Bp, GLA_HEADS, GLA_DK, GLA_DV), x_prompt.dtype)
        mix_p, s_p, c_p, kpe_p = mixer(hp, pos_p, s0, mla_prompt_attention, *mix_w)
        hp = layer_norm(ALPHA * hp + mix_p, ln1_g[l], ln1_b[l])
        mk_p, mv_p = memory_kv(mem_prompt, w_mk[l], w_mv[l])
        hp = layer_norm(ALPHA * hp + cross_attention(hp, mk_p, mv_p, w_xq[l], w_xo[l]), ln2_g[l], ln2_b[l])
        hp = layer_norm(ALPHA * hp + hier_moe(hp, *moe_w), ln3_g[l], ln3_b[l])
        past_c = cache_latent[l][page_table].reshape(Bd, past_len, MLA_KV_RANK)
        past_kpe = cache_krope[l][page_table].reshape(Bd, past_len, MLA_D_ROPE)
        attend_d = functools.partial(mla_sample_attention, past_c=past_c, past_kpe=past_kpe)
        mix_d, s_d, c_d, kpe_d = mixer(hd, pos_d, state_gla[l], attend_d, *mix_w)
        hd = layer_norm(ALPHA * hd + mix_d, ln1_g[l], ln1_b[l])
        hd = layer_norm(ALPHA * hd + cross_attention(hd, cache_mem_k[l], cache_mem_v[l], w_xq[l], w_xo[l]),
                        ln2_g[l], ln2_b[l])
        hd = layer_norm(ALPHA * hd + hier_moe(hd, *moe_w), ln3_g[l], ln3_b[l])
        sp_list.append(s_p); cp_list.append(c_p); kp_list.append(kpe_p)
        mk_list.append(mk_p); mv_list.append(mv_p)
        sd_list.append(s_d); cd_list.append(c_d); kd_list.append(kpe_d)
    return (hp, hd, jnp.stack(sp_list), jnp.stack(cp_list), jnp.stack(kp_list), jnp.stack(mk_list),
            jnp.stack(mv_list), jnp.stack(sd_list), jnp.stack(cd_list), jnp.stack(kd_list))
```

```python
import functools
import math

import jax
import jax.numpy as jnp
from jax import lax
from jax.experimental import pallas as pl
from jax.experimental.pallas import tpu as pltpu

F32 = jnp.float32
BF16 = jnp.bfloat16
HIGHEST = lax.Precision.HIGHEST

D_MODEL = 1024
GLA_HEADS = 4
GLA_DK = 64
GLA_DV = 128
GLA_GATE_RANK = 16
GLA_TAU = 16.0
MLA_HEADS = 8
MLA_DV = 64
MLA_D_NOPE = 64
MLA_D_ROPE = 32
MLA_Q_RANK = 384
MLA_KV_RANK = 256
MLA_SCALE = (MLA_D_NOPE + MLA_D_ROPE) ** -0.5
ROPE_THETA = 10000.0
X_HEADS = 4
X_DH = 256
N_GROUPS = 4
EXPERTS_PER_GROUP = 8
N_EXPERTS = N_GROUPS * EXPERTS_PER_GROUP
D_EXPERT = 256
DEPTH = 1
ALPHA = (2 * DEPTH) ** 0.25
LN_EPS = 1e-5
RMS_EPS = 1e-6

LANES = 128
GLA_QK = GLA_HEADS * GLA_DK
GLA_V = GLA_HEADS * GLA_DV
HEAD_BLOCK = LANES
Q_CAT = MLA_HEADS * HEAD_BLOCK
OFF_Q, OFF_K, OFF_V, OFF_R = 0, 256, 512, 1024
OFF_CQ, OFF_CKV, OFF_TAIL = 1536, 1920, 2176
D_IN_PACKED = OFF_TAIL + LANES
TAIL_A = MLA_D_ROPE
Q_EXT = MLA_KV_RANK + LANES
RTR_LANES = LANES
NEG_BIG = -1e30
VMEM_LIMIT = 56 * 1024 * 1024


def _dot(a, b, prec=None):
    return jnp.dot(a, b, preferred_element_type=F32, precision=prec)


def _dot_nt(a, b, prec=None):
    return lax.dot_general(a, b, (((1,), (1,)), ((), ())), preferred_element_type=F32, precision=prec)


def _dot_tn(a, b, prec=None):
    return lax.dot_general(a, b, (((0,), (0,)), ((), ())), preferred_element_type=F32, precision=prec)


def _rms(x, g):
    return x * lax.rsqrt(jnp.mean(x * x, axis=-1, keepdims=True) + RMS_EPS) * g


def _ln(x, g, b):
    mu = jnp.mean(x, axis=-1, keepdims=True)
    xc = x - mu
    var = jnp.mean(xc * xc, axis=-1, keepdims=True)
    return xc * lax.rsqrt(var + LN_EPS) * g + b


def _silu(x):
    return x / (1.0 + jnp.exp(-x))


def _log_sigmoid(x):
    return jnp.minimum(x, 0.0) - jnp.log(1.0 + jnp.exp(-jnp.abs(x)))


def _params(sem):
    return pltpu.CompilerParams(dimension_semantics=sem, vmem_limit_bytes=VMEM_LIMIT)


def _full(shape):
    return pl.BlockSpec(shape, lambda *_: (0,) * len(shape))


def _rope3(z, tab_ref, width):
    reps = width // LANES
    def t(i):
        v = tab_ref[i]
        return v if reps == 1 else jnp.concatenate([v] * reps, axis=1)
    half = MLA_D_ROPE // 2
    return (z * t(0) + pltpu.roll(z, width - half, 1) * t(1) + pltpu.roll(z, half, 1) * t(2))


def _mix_inputs(x, w_ref, wgate_ref, bgate_ref, gq_ref, gkv_ref, wuq_ref, ropeq_ref, ropek_ref, cdt, prec):
    z = _dot(x.astype(cdt), w_ref[...], prec)
    qg = z[:, OFF_Q:OFF_K] * (GLA_DK ** -0.5)
    kg = z[:, OFF_K:OFF_V]
    vg = z[:, OFF_V:OFF_R]
    rg = z[:, OFF_R:OFF_CQ]
    cq = z[:, OFF_CQ:OFF_CKV]
    ckv = z[:, OFF_CKV:OFF_TAIL]
    tail = z[:, OFF_TAIL:D_IN_PACKED]
    log_a = _log_sigmoid(_dot(tail.astype(cdt), wgate_ref[...], prec) + bgate_ref[...]) / GLA_TAU
    qn = _rms(cq, gq_ref[...])
    c_kv = _rms(ckv, gkv_ref[...])
    qcat = _rope3(_dot(qn.astype(cdt), wuq_ref[...], prec), ropeq_ref, Q_CAT)
    kt = _rope3(tail, ropek_ref, LANES)
    return qg, kg, vg, rg, log_a, c_kv, qcat, kt


def _inproj_prompt_kernel(x_ref, w_ref, wgate_ref, bgate_ref, gq_ref, gkv_ref, wuq_ref, ropeq_ref,
                          ropek_ref, wk_ref, wuv_ref,
                          qg_ref, kg_ref, la_ref, vg_ref, rg_ref, lat_ref, kpe_ref, qcat_ref, kcat_ref,
                          vm_ref):
    qg, kg, vg, rg, log_a, c_kv, qcat, kt = _mix_inputs(
        x_ref[...], w_ref, wgate_ref, bgate_ref, gq_ref, gkv_ref, wuq_ref, ropeq_ref, ropek_ref, BF16, None)
    qg_ref[...] = qg
    kg_ref[...] = kg
    la_ref[...] = log_a
    vg_ref[...] = vg.astype(BF16)
    rg_ref[...] = rg
    lat_ref[...] = c_kv
    kpe_ref[...] = kt[:, :MLA_D_ROPE]
    qcat_ref[...] = qcat.astype(BF16)
    cb = c_kv.astype(BF16)
    kcat_ref[...] = _dot(jnp.concatenate([cb, kt.astype(BF16)], axis=1), wk_ref[...]).astype(BF16)
    vm_ref[...] = _dot(cb, wuv_ref[...]).astype(BF16)


def _inproj_sample_kernel(x_ref, w_ref, wgate_ref, bgate_ref, gq_ref, gkv_ref, wuq_ref, ropeq_ref,
                          ropek_ref, wabs_ref,
                          qg_ref, kg_ref, la_ref, vg_ref, rg_ref, lat_ref, kpe_ref, qext_ref):
    qg, kg, vg, rg, log_a, c_kv, qcat, kt = _mix_inputs(
        x_ref[...], w_ref, wgate_ref, bgate_ref, gq_ref, gkv_ref, wuq_ref, ropeq_ref, ropek_ref, F32, HIGHEST)
    qg_ref[...] = qg
    kg_ref[...] = kg
    la_ref[...] = log_a
    vg_ref[...] = vg
    rg_ref[...] = rg
    lat_ref[...] = c_kv
    kpe_ref[...] = kt[:, :MLA_D_ROPE]
    for h in range(MLA_HEADS):
        qext_ref[:, h * Q_EXT:(h + 1) * Q_EXT] = _dot(
            qcat[:, h * HEAD_BLOCK:(h + 1) * HEAD_BLOCK], wabs_ref[h], HIGHEST)


def _gla_prompt_kernel(qg_ref, kg_ref, la_ref, vg_ref, rg_ref, gn_ref, o_ref, s_ref, *, chunk):
    c_idx = pl.program_id(1)

    @pl.when(c_idx == 0)
    def _():
        s_ref[...] = jnp.zeros_like(s_ref)

    g = la_ref[...]
    row = lax.broadcasted_iota(jnp.int32, (chunk, chunk), 0)
    col = lax.broadcasted_iota(jnp.int32, (chunk, chunk), 1)
    causal = col <= row
    b = _dot(causal.astype(F32), g, HIGHEST)
    b_last = b[chunk - 1:chunk, :]
    b_mid = b[chunk // 2 - 1:chunk // 2, :]
    qg = qg_ref[...]
    kg = kg_ref[...]
    q_rel = (qg * jnp.exp(b - b_mid)).astype(BF16)
    k_rel = (kg * jnp.exp(b_mid - b)).astype(BF16)
    q_in = (qg * jnp.exp(b)).astype(BF16)
    k_out = (kg * jnp.exp(b_last - b)).astype(BF16)
    ones = jnp.ones((chunk, GLA_DV), F32)
    gn = gn_ref[...]
    for h in range(GLA_HEADS):
        ks = slice(h * GLA_DK, (h + 1) * GLA_DK)
        vs = slice(h * GLA_DV, (h + 1) * GLA_DV)
        v_h = vg_ref[:, vs]
        s_h = s_ref[0, h]
        attn = jnp.where(causal, _dot_nt(q_rel[:, ks], k_rel[:, ks]), 0.0)
        o = _dot(attn.astype(BF16), v_h) + _dot(q_in[:, ks], s_h.astype(BF16))
        decay_col = jnp.exp(_dot_tn(g[:, ks], ones, HIGHEST))
        s_ref[0, h] = decay_col * s_h + _dot_tn(k_out[:, ks], v_h)
        o_ref[:, vs] = (_rms(o, gn) * _silu(rg_ref[:, vs])).astype(o_ref.dtype)


def _mla_prompt_kernel(q_ref, k_ref, v_ref, o_ref, *, tq):
    iq = pl.program_id(2)
    outs = []
    for hh in range(2):
        ls = slice(hh * HEAD_BLOCK, (hh + 1) * HEAD_BLOCK)
        q = q_ref[0, :, ls]

        def block(kb, carry, masked):
            m, l, acc = carry
            start = pl.multiple_of(kb * tq, tq)
            k = k_ref[0, pl.ds(start, tq), ls]
            v = v_ref[0, pl.ds(start, tq), :]
            s = _dot_nt(q, k) * MLA_SCALE
            if masked:
                row = lax.broadcasted_iota(jnp.int32, (tq, tq), 0)
                col = lax.broadcasted_iota(jnp.int32, (tq, tq), 1)
                s = jnp.where(col <= row, s, NEG_BIG)
            m_new = jnp.maximum(m, jnp.max(s, axis=1, keepdims=True))
            a = jnp.exp(m - m_new)
            p = jnp.exp(s - m_new)
            l = a * l + jnp.sum(p, axis=1, keepdims=True)
            acc = a * acc + _dot(p.astype(BF16), v)
            return m_new, l, acc

        init = (jnp.full((tq, 1), NEG_BIG, F32), jnp.zeros((tq, 1), F32), jnp.zeros((tq, HEAD_BLOCK), F32))
        carry = lax.fori_loop(0, iq, functools.partial(block, masked=False), init)
        m, l, acc = block(iq, carry, True)
        outs.append(acc / l)
    lane = lax.broadcasted_iota(jnp.int32, (tq, HEAD_BLOCK), 1)
    o_ref[0] = jnp.where(lane < MLA_DV, outs[0], outs[1]).astype(o_ref.dtype)


def _memkv_kernel(x_ref, wk_ref, wv_ref, k_ref, v_ref):
    xb = x_ref[...].astype(BF16)
    k_ref[...] = _dot(xb, wk_ref[...])
    v_ref[...] = _dot(xb, wv_ref[...])


def _post_prompt_kernel(x_ref, go_ref, mo_ref, wout_ref, g1_ref, b1_ref, wxq_ref, mk_ref, mv_ref, wxo_ref,
                        g2_ref, b2_ref, wrt_ref, h2_ref, lg_ref):
    mix = _dot(go_ref[...], wout_ref[:GLA_V, :]) + _dot(mo_ref[...], wout_ref[GLA_V:, :])
    h1 = _ln(ALPHA * x_ref[...] + mix, g1_ref[...], b1_ref[...])
    qx = _dot(h1.astype(BF16), wxq_ref[...]).astype(BF16)
    outs = []
    for h in range(X_HEADS):
        hs = slice(h * X_DH, (h + 1) * X_DH)
        s = _dot_nt(qx[:, hs], mk_ref[:, hs].astype(BF16)) * (X_DH ** -0.5)
        s = s - jnp.max(s, axis=1, keepdims=True)
        p = jnp.exp(s)
        p = p / jnp.sum(p, axis=1, keepdims=True)
        outs.append(_dot(p.astype(BF16), mv_ref[:, hs].astype(BF16)))
    ox = jnp.concatenate(outs, axis=1)
    h2 = _ln(ALPHA * h1 + _dot(ox.astype(BF16), wxo_ref[...]), g2_ref[...], b2_ref[...])
    h2_ref[...] = h2
    lg_ref[...] = _dot(h2, wrt_ref[...], HIGHEST)


def _gates_from_logits(lg, bias):
    lane = lax.broadcasted_iota(jnp.int32, lg.shape, 1)
    neg = jnp.float32(-jnp.inf)

    def first_argmax(v):
        mx = jnp.max(v, axis=1, keepdims=True)
        return jnp.min(jnp.where(v == mx, lane, RTR_LANES), axis=1, keepdims=True)

    def pick(v, idx):
        return jnp.sum(jnp.where(lane == idx, v, 0.0), axis=1, keepdims=True)

    is_grp = lane < N_GROUPS
    g_idx = first_argmax(jnp.where(is_grp, lg + bias, neg))
    g_max = jnp.max(jnp.where(is_grp, lg, neg), axis=1, keepdims=True)
    g_den = jnp.sum(jnp.where(is_grp, jnp.exp(lg - g_max), 0.0), axis=1, keepdims=True)
    p_g = jnp.exp(pick(lg, g_idx) - g_max) / g_den
    lo = N_GROUPS + g_idx * EXPERTS_PER_GROUP
    in_grp = (lane >= lo) & (lane < lo + EXPERTS_PER_GROUP)
    biased = jnp.where(in_grp, lg + bias, neg)
    i1 = first_argmax(biased)
    i2 = first_argmax(jnp.where(lane == i1, neg, biased))
    l1 = pick(lg, i1)
    l2 = pick(lg, i2)
    mx = jnp.maximum(l1, l2)
    e1 = jnp.exp(l1 - mx)
    e2 = jnp.exp(l2 - mx)
    den = e1 + e2
    return jnp.where(lane == i1, p_g * (e1 / den), 0.0) + jnp.where(lane == i2, p_g * (e2 / den), 0.0)


def _moe_kernel(h_ref, lg_ref, brt_ref, wg_ref, wu_ref, wd_ref, g3_ref, b3_ref, o_ref,
                acc_ref, gate_ref, hb_ref):
    e = pl.program_id(1)

    @pl.when(e == 0)
    def _():
        acc_ref[...] = jnp.zeros_like(acc_ref)
        gate_ref[...] = _gates_from_logits(lg_ref[...], brt_ref[...])
        hb_ref[...] = h_ref[...].astype(BF16)

    hb = hb_ref[...]
    he = _silu(_dot(hb, wg_ref[0])) * _dot(hb, wu_ref[0])
    y = _dot(he.astype(BF16), wd_ref[0])
    gates = gate_ref[...]
    lane = lax.broadcasted_iota(jnp.int32, gates.shape, 1)
    gcol = jnp.sum(jnp.where(lane == e + N_GROUPS, gates, 0.0), axis=1, keepdims=True)
    acc_ref[...] += gcol * y

    @pl.when(e == pl.num_programs(1) - 1)
    def _():
        o_ref[...] = _ln(ALPHA * h_ref[...] + acc_ref[...], g3_ref[...], b3_ref[...])


def _mla_sample_kernel(pt_ref, qext_ref, *refs, pages):
    lat_refs = refs[:pages]
    kr_refs = refs[pages:2 * pages]
    st_ref = refs[2 * pages]
    j = pl.program_id(1)
    r = MLA_KV_RANK

    @pl.when(j == 0)
    def _():
        st_ref[0, :r + 1, :] = jnp.zeros((r + 1, MLA_HEADS), F32)
        st_ref[0, r:r + 1, :] = jnp.full((1, MLA_HEADS), NEG_BIG, F32)
        st_ref[0, r + 1:, :] = jnp.zeros((st_ref.shape[1] - r - 1, MLA_HEADS), F32)

    q = qext_ref[0]
    c = jnp.concatenate([ref[0, 0] for ref in lat_refs], axis=0)
    kp = jnp.concatenate([ref[0, 0] for ref in kr_refs], axis=0)
    s = (_dot_nt(c, q[:, :r], HIGHEST) + _dot_nt(kp, q[:, r:r + MLA_D_ROPE], HIGHEST)) * MLA_SCALE
    m_old = st_ref[0, r:r + 1, :]
    l_old = st_ref[0, r + 1:r + 2, :]
    m_new = jnp.maximum(m_old, jnp.max(s, axis=0, keepdims=True))
    a = jnp.exp(m_old - m_new)
    p = jnp.exp(s - m_new)
    st_ref[0, :r, :] = st_ref[0, :r, :] * a + _dot_tn(c, p, HIGHEST)
    st_ref[0, r:r + 1, :] = m_new
    st_ref[0, r + 1:r + 2, :] = a * l_old + jnp.sum(p, axis=0, keepdims=True)


def _gla_sample_kernel(qg_ref, kg_ref, la_ref, vg_ref, rg_ref, gn_ref, s0_ref, o_ref, s1_ref):
    decay = jnp.exp(la_ref[...])
    qg = qg_ref[...]
    kg = kg_ref[...]
    gn = gn_ref[...]
    for h in range(GLA_HEADS):
        vs = slice(h * GLA_DV, (h + 1) * GLA_DV)
        v_h = vg_ref[:, vs]
        o = jnp.zeros(v_h.shape, F32)
        for d in range(GLA_DK):
            c = h * GLA_DK + d
            ss = slice(c * GLA_DV, (c + 1) * GLA_DV)
            s_new = decay[:, c:c + 1] * s0_ref[:, ss] + kg[:, c:c + 1] * v_h
            s1_ref[:, ss] = s_new
            o = o + qg[:, c:c + 1] * s_new
        o_ref[:, vs] = _rms(o, gn) * _silu(rg_ref[:, vs])


def _mid_sample_kernel(x_ref, go_ref, acc_ref, m_ref, l_ref, qext_ref, lat_ref, kpe_ref, wuv_ref, wout_ref,
                       g1_ref, b1_ref, wxq_ref, h1_ref, qx_ref):
    r = MLA_KV_RANK
    lat = lat_ref[...]
    kpe = kpe_ref[...]
    parts = []
    for h in range(MLA_HEADS):
        q_lat = qext_ref[:, h * Q_EXT:h * Q_EXT + r]
        q_pe = qext_ref[:, h * Q_EXT + r:h * Q_EXT + r + MLA_D_ROPE]
        s_new = (jnp.sum(q_lat * lat, axis=1, keepdims=True)
                 + jnp.sum(q_pe * kpe, axis=1, keepdims=True)) * MLA_SCALE
        m_old = m_ref[:, h:h + 1]
        m_new = jnp.maximum(m_old, s_new)
        a = jnp.exp(m_old - m_new)
        p_new = jnp.exp(s_new - m_new)
        den = a * l_ref[:, h:h + 1] + p_new
        parts.append((acc_ref[:, h * r:(h + 1) * r] * a + p_new * lat) / den)
    mo = _dot(jnp.concatenate(parts, axis=1), wuv_ref[...], HIGHEST)
    mix = _dot(go_ref[...], wout_ref[:GLA_V, :], HIGHEST) + _dot(mo, wout_ref[GLA_V:, :], HIGHEST)
    h1 = _ln(ALPHA * x_ref[...] + mix, g1_ref[...], b1_ref[...])
    h1_ref[...] = h1
    qx_ref[...] = _dot(h1, wxq_ref[...], HIGHEST)


def _xattn_sample_kernel(q_ref, k_ref, v_ref, o_ref):
    q = q_ref[0]
    for h in range(X_HEADS):
        hs = slice(h * X_DH, (h + 1) * X_DH)
        k = k_ref[0, 0, :, h, :]
        v = v_ref[0, 0, :, h, :]
        s = jnp.sum(k * q[:, hs], axis=1, keepdims=True) * (X_DH ** -0.5)
        p = jnp.exp(s - jnp.max(s, axis=0, keepdims=True))
        o = jnp.sum(p * v, axis=0, keepdims=True) / jnp.sum(p, axis=0, keepdims=True)
        o_ref[0, :, hs] = o


def _post_sample_kernel(h1_ref, ox_ref, wxo_ref, g2_ref, b2_ref, wrt_ref, h2_ref, lg_ref):
    h2 = _ln(ALPHA * h1_ref[...] + _dot(ox_ref[...], wxo_ref[...], HIGHEST), g2_ref[...], b2_ref[...])
    h2_ref[...] = h2
    lg_ref[...] = _dot(h2, wrt_ref[...], HIGHEST)


def _pack_w_in(w_in):
    sizes = (GLA_QK, GLA_QK, GLA_V, GLA_GATE_RANK, GLA_V, MLA_Q_RANK, MLA_KV_RANK, MLA_D_ROPE)
    offs = [0]
    for s in sizes:
        offs.append(offs[-1] + s)
    q, k, v, a, r, cq, ckv, kr = (w_in[:, offs[i]:offs[i + 1]] for i in range(8))
    pad = jnp.zeros((w_in.shape[0], LANES - MLA_D_ROPE - GLA_GATE_RANK), w_in.dtype)
    return jnp.concatenate([q, k, v, r, cq, ckv, kr, a, pad], axis=1)


def _pack_gate(w_gla_gate):
    z = jnp.zeros((LANES, GLA_QK), w_gla_gate.dtype)
    return z.at[TAIL_A:TAIL_A + GLA_GATE_RANK].set(w_gla_gate)


def _pack_wuq(w_uq):
    pad = jnp.zeros(w_uq.shape[:2] + (HEAD_BLOCK - MLA_D_NOPE - MLA_D_ROPE,), w_uq.dtype)
    return jnp.concatenate([w_uq, pad], axis=2).reshape(w_uq.shape[0], Q_CAT)


def _pack_wk(w_uk):
    top = jnp.concatenate([w_uk, jnp.zeros(w_uk.shape[:2] + (HEAD_BLOCK - MLA_D_NOPE,), w_uk.dtype)], axis=2)
    place = jnp.zeros((LANES, HEAD_BLOCK), w_uk.dtype)
    place = place.at[jnp.arange(MLA_D_ROPE), MLA_D_NOPE + jnp.arange(MLA_D_ROPE)].set(1.0)
    bottom = jnp.tile(place[:, None, :], (1, MLA_HEADS, 1))
    return jnp.concatenate([top, bottom], axis=0).reshape(MLA_KV_RANK + LANES, Q_CAT)


def _pack_wabs(w_uk):
    w = jnp.zeros((MLA_HEADS, HEAD_BLOCK, Q_EXT), w_uk.dtype)
    w = w.at[:, :MLA_D_NOPE, :MLA_KV_RANK].set(jnp.transpose(w_uk, (1, 2, 0)))
    idx = jnp.arange(MLA_D_ROPE)
    return w.at[:, MLA_D_NOPE + idx, MLA_KV_RANK + idx].set(1.0)


def _pack_wuv_blockdiag(w_uv):
    w = jnp.zeros((MLA_HEADS, MLA_KV_RANK, MLA_HEADS, MLA_DV), w_uv.dtype)
    for h in range(MLA_HEADS):
        w = w.at[h, :, h, :].set(w_uv[:, h, :])
    return w.reshape(MLA_HEADS * MLA_KV_RANK, MLA_HEADS * MLA_DV)


def _pack_router(w_grp, w_rtr, b_grp, b_rtr):
    pad = RTR_LANES - N_GROUPS - N_EXPERTS
    w = jnp.concatenate([w_grp, w_rtr, jnp.zeros((w_grp.shape[0], pad), w_grp.dtype)], axis=1)
    b = jnp.concatenate([b_grp, b_rtr, jnp.zeros((pad,), b_grp.dtype)])[None, :]
    return w, b


def _rope_tables(positions):
    half = MLA_D_ROPE // 2
    inv = ROPE_THETA ** (-jnp.arange(0, MLA_D_ROPE, 2, dtype=F32) / MLA_D_ROPE)
    ang = positions.astype(F32)[:, None] * inv[None, :]
    cos, sin = jnp.cos(ang), jnp.sin(ang)
    n = positions.shape[0]
    z = lambda w: jnp.zeros((n, w), F32)
    one = jnp.ones((n, MLA_D_NOPE), F32)
    tail = HEAD_BLOCK - MLA_D_NOPE - MLA_D_ROPE
    tq = jnp.stack([jnp.concatenate([one, cos, cos, z(tail)], axis=1),
                    jnp.concatenate([z(MLA_D_NOPE), -sin, z(half), z(tail)], axis=1),
                    jnp.concatenate([z(MLA_D_NOPE), z(half), sin, z(tail)], axis=1)])
    rest = LANES - MLA_D_ROPE
    tk = jnp.stack([jnp.concatenate([cos, cos, z(rest)], axis=1),
                    jnp.concatenate([-sin, z(half), z(rest)], axis=1),
                    jnp.concatenate([z(half), sin, z(rest)], axis=1)])
    return tq, tk


def _row_tile(n, target):
    t = math.gcd(n, target)
    assert t % 8 == 0 or t == n
    return t


def _moe(h2, logits, b_rt, wg, wu, wd, g3, b3, tm):
    n = h2.shape[0]
    return pl.pallas_call(
        _moe_kernel,
        grid=(n // tm, N_EXPERTS),
        in_specs=[
            pl.BlockSpec((tm, D_MODEL), lambda i, e: (i, 0)),
            pl.BlockSpec((tm, RTR_LANES), lambda i, e: (i, 0)),
            pl.BlockSpec((1, RTR_LANES), lambda i, e: (0, 0)),
            pl.BlockSpec((1, D_MODEL, D_EXPERT), lambda i, e: (e, 0, 0)),
            pl.BlockSpec((1, D_MODEL, D_EXPERT), lambda i, e: (e, 0, 0)),
            pl.BlockSpec((1, D_EXPERT, D_MODEL), lambda i, e: (e, 0, 0)),
            pl.BlockSpec((1, D_MODEL), lambda i, e: (0, 0)),
            pl.BlockSpec((1, D_MODEL), lambda i, e: (0, 0)),
        ],
        out_specs=pl.BlockSpec((tm, D_MODEL), lambda i, e: (i, 0)),
        out_shape=jax.ShapeDtypeStruct((n, D_MODEL), F32),
        scratch_shapes=[pltpu.VMEM((tm, D_MODEL), F32), pltpu.VMEM((tm, RTR_LANES), F32),
                        pltpu.VMEM((tm, D_MODEL), BF16)],
        compiler_params=_params(("parallel", "arbitrary")),
        name="moe",
    )(h2, logits, b_rt, wg, wu, wd, g3, b3)


def _prompt_group(x_prompt, mem_prompt, wp, tabs):
    bp, lp, _ = x_prompt.shape
    n = bp * lp
    x2 = x_prompt.reshape(n, D_MODEL)
    tm = _row_tile(lp, 512)
    per_row = lp // tm
    ropeq, ropek = tabs
    row = lambda w: pl.BlockSpec((tm, w), lambda i: (i, 0))
    rope_spec = pl.BlockSpec((3, tm, LANES), lambda i: (0, i % per_row, 0))
    outs = pl.pallas_call(
        _inproj_prompt_kernel,
        grid=(n // tm,),
        in_specs=[row(D_MODEL), _full((D_MODEL, D_IN_PACKED)), _full((LANES, GLA_QK)), _full((1, GLA_QK)),
                  _full((1, MLA_Q_RANK)), _full((1, MLA_KV_RANK)), _full((MLA_Q_RANK, Q_CAT)),
                  rope_spec, rope_spec, _full((MLA_KV_RANK + LANES, Q_CAT)),
                  _full((MLA_KV_RANK, MLA_HEADS * MLA_DV))],
        out_specs=[row(GLA_QK), row(GLA_QK), row(GLA_QK), row(GLA_V), row(GLA_V), row(MLA_KV_RANK),
                   row(MLA_D_ROPE), row(Q_CAT), row(Q_CAT), row(MLA_HEADS * MLA_DV)],
        out_shape=[jax.ShapeDtypeStruct((n, GLA_QK), F32), jax.ShapeDtypeStruct((n, GLA_QK), F32),
                   jax.ShapeDtypeStruct((n, GLA_QK), F32), jax.ShapeDtypeStruct((n, GLA_V), BF16),
                   jax.ShapeDtypeStruct((n, GLA_V), F32), jax.ShapeDtypeStruct((n, MLA_KV_RANK), F32),
                   jax.ShapeDtypeStruct((n, MLA_D_ROPE), F32), jax.ShapeDtypeStruct((n, Q_CAT), BF16),
                   jax.ShapeDtypeStruct((n, Q_CAT), BF16), jax.ShapeDtypeStruct((n, MLA_HEADS * MLA_DV), BF16)],
        compiler_params=_params(("parallel",)),
        name="inproj_prompt",
    )(x2, wp["w_in_b"], wp["w_gate_b"], wp["b_gate"], wp["gq"], wp["gkv"], wp["w_uq_b"], ropeq, ropek,
      wp["w_k_b"], wp["w_uv_b"])
    qg, kg, la, vg, rg, lat, kpe, qcat, kcat, vm = outs

    chunk = _row_tile(lp, 128)
    nc = lp // chunk
    crow = lambda w: pl.BlockSpec((chunk, w), lambda b, c: (b * nc + c, 0))
    gla_o, s_p = pl.pallas_call(
        functools.partial(_gla_prompt_kernel, chunk=chunk),
        grid=(bp, nc),
        in_specs=[crow(GLA_QK), crow(GLA_QK), crow(GLA_QK), crow(GLA_V), crow(GLA_V), _full((1, GLA_DV))],
        out_specs=[crow(GLA_V), pl.BlockSpec((1, GLA_HEADS, GLA_DK, GLA_DV), lambda b, c: (b, 0, 0, 0))],
        out_shape=[jax.ShapeDtypeStruct((n, GLA_V), BF16),
                   jax.ShapeDtypeStruct((bp, GLA_HEADS, GLA_DK, GLA_DV), F32)],
        compiler_params=_params(("parallel", "arbitrary")),
        name="gla_prompt",
    )(qg, kg, la, vg, rg, wp["gn"])

    tq = _row_tile(lp, 256)
    mla_o = pl.pallas_call(
        functools.partial(_mla_prompt_kernel, tq=tq),
        grid=(bp, MLA_HEADS // 2, lp // tq),
        in_specs=[pl.BlockSpec((1, tq, 2 * HEAD_BLOCK), lambda b, p, i: (b, i, p)),
                  pl.BlockSpec((1, lp, 2 * HEAD_BLOCK), lambda b, p, i: (b, 0, p)),
                  pl.BlockSpec((1, lp, 2 * MLA_DV), lambda b, p, i: (b, 0, p))],
        out_specs=pl.BlockSpec((1, tq, 2 * MLA_DV), lambda b, p, i: (b, i, p)),
        out_shape=jax.ShapeDtypeStruct((bp, lp, MLA_HEADS * MLA_DV), BF16),
        compiler_params=_params(("parallel", "parallel", "arbitrary")),
        name="mla_prompt",
    )(qcat.reshape(bp, lp, Q_CAT), kcat.reshape(bp, lp, Q_CAT), vm.reshape(bp, lp, MLA_HEADS * MLA_DV))
    mla_o = mla_o.reshape(n, MLA_HEADS * MLA_DV)

    nm = mem_prompt.shape[1]
    mem2 = mem_prompt.reshape(bp * nm, D_MODEL)
    tmm = _row_tile(bp * nm, 512)
    xd = X_HEADS * X_DH
    mk, mv = pl.pallas_call(
        _memkv_kernel,
        grid=(bp * nm // tmm,),
        in_specs=[pl.BlockSpec((tmm, D_MODEL), lambda i: (i, 0)), _full((D_MODEL, xd)), _full((D_MODEL, xd))],
        out_specs=[pl.BlockSpec((tmm, xd), lambda i: (i, 0))] * 2,
        out_shape=[jax.ShapeDtypeStruct((bp * nm, xd), F32)] * 2,
        compiler_params=_params(("parallel",)),
        name="memkv",
    )(mem2, wp["w_mk_b"], wp["w_mv_b"])

    vec = _full((1, D_MODEL))
    mem_spec = pl.BlockSpec((nm, xd), lambda i: (i // per_row, 0))
    h2, logits = pl.pallas_call(
        _post_prompt_kernel,
        grid=(n // tm,),
        in_specs=[row(D_MODEL), row(GLA_V), row(MLA_HEADS * MLA_DV), _full((D_MODEL, D_MODEL)), vec, vec,
                  _full((D_MODEL, xd)), mem_spec, mem_spec, _full((xd, D_MODEL)), vec, vec,
                  _full((D_MODEL, RTR_LANES))],
        out_specs=[row(D_MODEL), row(RTR_LANES)],
        out_shape=[jax.ShapeDtypeStruct((n, D_MODEL), F32), jax.ShapeDtypeStruct((n, RTR_LANES), F32)],
        compiler_params=_params(("parallel",)),
        name="post_prompt",
    )(x2, gla_o, mla_o, wp["w_out_b"], wp["g1"], wp["b1"], wp["w_xq_b"], mk, mv, wp["w_xo_b"],
      wp["g2"], wp["b2"], wp["w_rt"])

    y = _moe(h2, logits, wp["b_rt"], wp["w_eg_b"], wp["w_eu_b"], wp["w_ed_b"], wp["g3"], wp["b3"],
             _row_tile(n, 1024))
    return (y.reshape(bp, lp, D_MODEL), s_p[None], lat.reshape(1, bp, lp, MLA_KV_RANK),
            kpe.reshape(1, bp, lp, MLA_D_ROPE), mk.reshape(1, bp, nm, X_HEADS, X_DH),
            mv.reshape(1, bp, nm, X_HEADS, X_DH))


def _sample_group(x_sample, state_gla, cache_latent, cache_krope, cache_mem_k, cache_mem_v, page_table,
                  wp, tabs):
    bd = x_sample.shape[0]
    x2 = x_sample.reshape(bd, D_MODEL)
    ropeq, ropek = tabs
    n_pages = page_table.shape[1]
    page = cache_latent.shape[2]
    full2 = lambda w: _full((bd, w))
    outs = pl.pallas_call(
        _inproj_sample_kernel,
        grid=(1,),
        in_specs=[full2(D_MODEL), _full((D_MODEL, D_IN_PACKED)), _full((LANES, GLA_QK)), _full((1, GLA_QK)),
                  _full((1, MLA_Q_RANK)), _full((1, MLA_KV_RANK)), _full((MLA_Q_RANK, Q_CAT)),
                  _full((3, 1, LANES)), _full((3, 1, LANES)), _full((MLA_HEADS, HEAD_BLOCK, Q_EXT))],
        out_specs=[full2(GLA_QK), full2(GLA_QK), full2(GLA_QK), full2(GLA_V), full2(GLA_V),
                   full2(MLA_KV_RANK), full2(MLA_D_ROPE), full2(MLA_HEADS * Q_EXT)],
        out_shape=[jax.ShapeDtypeStruct((bd, w), F32) for w in
                   (GLA_QK, GLA_QK, GLA_QK, GLA_V, GLA_V, MLA_KV_RANK, MLA_D_ROPE, MLA_HEADS * Q_EXT)],
        compiler_params=_params(("arbitrary",)),
        name="inproj_sample",
    )(x2, wp["w_in"], wp["w_gate"], wp["b_gate"], wp["gq"], wp["gkv"], wp["w_uq"], ropeq, ropek, wp["w_abs"])
    qg, kg, la, vg, rg, lat, kpe, qext = outs

    pages = math.gcd(n_pages, 16)
    st_rows = MLA_KV_RANK + 8
    lat_specs = [pl.BlockSpec((1, 1, page, MLA_KV_RANK),
                              functools.partial(lambda b, j, pt, p: (0, pt[b, j * pages + p], 0, 0), p=p))
                 for p in range(pages)]
    kr_specs = [pl.BlockSpec((1, 1, page, MLA_D_ROPE),
                             functools.partial(lambda b, j, pt, p: (0, pt[b, j * pages + p], 0, 0), p=p))
                for p in range(pages)]
    st = pl.pallas_call(
        functools.partial(_mla_sample_kernel, pages=pages),
        grid_spec=pltpu.PrefetchScalarGridSpec(
            num_scalar_prefetch=1,
            grid=(bd, n_pages // pages),
            in_specs=[pl.BlockSpec((1, MLA_HEADS, Q_EXT), lambda b, j, pt: (b, 0, 0))] + lat_specs + kr_specs,
            out_specs=pl.BlockSpec((1, st_rows, MLA_HEADS), lambda b, j, pt: (b, 0, 0)),
        ),
        out_shape=jax.ShapeDtypeStruct((bd, st_rows, MLA_HEADS), F32),
        compiler_params=_params(("parallel", "arbitrary")),
        name="mla_sample",
    )(page_table, qext.reshape(bd, MLA_HEADS, Q_EXT), *([cache_latent] * pages), *([cache_krope] * pages))
    acc = jnp.swapaxes(st[:, :MLA_KV_RANK, :], 1, 2).reshape(bd, MLA_HEADS * MLA_KV_RANK)
    m_run = st[:, MLA_KV_RANK, :]
    l_run = st[:, MLA_KV_RANK + 1, :]

    s_flat = GLA_HEADS * GLA_DK * GLA_DV
    bb = _row_tile(bd, 32)
    brow = lambda w: pl.BlockSpec((bb, w), lambda i: (i, 0))
    gla_o, s_d = pl.pallas_call(
        _gla_sample_kernel,
        grid=(bd // bb,),
        in_specs=[brow(GLA_QK), brow(GLA_QK), brow(GLA_QK), brow(GLA_V), brow(GLA_V), _full((1, GLA_DV)),
                  brow(s_flat)],
        out_specs=[brow(GLA_V), brow(s_flat)],
        out_shape=[jax.ShapeDtypeStruct((bd, GLA_V), F32), jax.ShapeDtypeStruct((bd, s_flat), F32)],
        compiler_params=_params(("parallel",)),
        name="gla_sample",
    )(qg, kg, la, vg, rg, wp["gn"], state_gla.reshape(bd, s_flat))

    vec = _full((1, D_MODEL))
    xd = X_HEADS * X_DH
    h1, qx = pl.pallas_call(
        _mid_sample_kernel,
        grid=(1,),
        in_specs=[full2(D_MODEL), full2(GLA_V), full2(MLA_HEADS * MLA_KV_RANK), full2(MLA_HEADS),
                  full2(MLA_HEADS), full2(MLA_HEADS * Q_EXT), full2(MLA_KV_RANK), full2(MLA_D_ROPE),
                  _full((MLA_HEADS * MLA_KV_RANK, MLA_HEADS * MLA_DV)), _full((D_MODEL, D_MODEL)), vec, vec,
                  _full((D_MODEL, xd))],
        out_specs=[full2(D_MODEL), full2(xd)],
        out_shape=[jax.ShapeDtypeStruct((bd, D_MODEL), F32), jax.ShapeDtypeStruct((bd, xd), F32)],
        compiler_params=_params(("arbitrary",)),
        name="mid_sample",
    )(x2, gla_o, acc, m_run, l_run, qext, lat, kpe, wp["w_uv_bd"], wp["w_out"], wp["g1"], wp["b1"], wp["w_xq"])

    nm = cache_mem_k.shape[2]
    mem_spec = pl.BlockSpec((1, 1, nm, X_HEADS, X_DH), lambda b: (0, b, 0, 0, 0))
    ox = pl.pallas_call(
        _xattn_sample_kernel,
        grid=(bd,),
        in_specs=[pl.BlockSpec((1, 1, xd), lambda b: (b, 0, 0)), mem_spec, mem_spec],
        out_specs=pl.BlockSpec((1, 1, xd), lambda b: (b, 0, 0)),
        out_shape=jax.ShapeDtypeStruct((bd, 1, xd), F32),
        compiler_params=_params(("parallel",)),
        name="xattn_sample",
    )(qx.reshape(bd, 1, xd), cache_mem_k, cache_mem_v)

    h2, logits = pl.pallas_call(
        _post_sample_kernel,
        grid=(1,),
        in_specs=[full2(D_MODEL), full2(xd), _full((xd, D_MODEL)), vec, vec, _full((D_MODEL, RTR_LANES))],
        out_specs=[full2(D_MODEL), full2(RTR_LANES)],
        out_shape=[jax.ShapeDtypeStruct((bd, D_MODEL), F32), jax.ShapeDtypeStruct((bd, RTR_LANES), F32)],
        compiler_params=_params(("arbitrary",)),
        name="post_sample",
    )(h1, ox.reshape(bd, xd), wp["w_xo"], wp["g2"], wp["b2"], wp["w_rt"])

    y = _moe(h2, logits, wp["b_rt"], wp["w_eg_b"], wp["w_eu_b"], wp["w_ed_b"], wp["g3"], wp["b3"], bd)
    return (y.reshape(bd, 1, D_MODEL), s_d.reshape(1, bd, GLA_HEADS, GLA_DK, GLA_DV),
            lat.reshape(1, bd, 1, MLA_KV_RANK), kpe.reshape(1, bd, 1, MLA_D_ROPE))


def kernel(x_prompt, x_sample, mem_prompt, state_gla, cache_latent, cache_krope, cache_mem_k, cache_mem_v,
           page_table, w_in, w_gla_gate, b_gla_gate, gla_norm_g, mla_q_norm_g, w_uq, mla_kv_norm_g, w_uk,
           w_uv, w_out, ln1_g, ln1_b, w_xq, w_mk, w_mv, w_xo, ln2_g, ln2_b, w_grp, b_grp, w_rtr, b_rtr,
           w_e_gate, w_e_up, w_e_down, ln3_g, ln3_b):
    assert w_in.shape[0] == DEPTH == 1 and x_sample.shape[1] == 1
    l = 0
    b16 = lambda a: a.astype(BF16)
    w_in_p = _pack_w_in(w_in[l])
    w_gate_p = _pack_gate(w_gla_gate[l])
    w_uq_p = _pack_wuq(w_uq[l])
    w_rt, b_rt = _pack_router(w_grp[l], w_rtr[l], b_grp[l], b_rtr[l])
    wp = dict(
        w_in=w_in_p, w_in_b=b16(w_in_p), w_gate=w_gate_p, w_gate_b=b16(w_gate_p),
        b_gate=b_gla_gate[l][None, :], gq=mla_q_norm_g[l][None, :], gkv=mla_kv_norm_g[l][None, :],
        gn=gla_norm_g[l][None, :], w_uq=w_uq_p, w_uq_b=b16(w_uq_p), w_k_b=b16(_pack_wk(w_uk[l])),
        w_uv_b=b16(w_uv[l].reshape(MLA_KV_RANK, MLA_HEADS * MLA_DV)), w_abs=_pack_wabs(w_uk[l]),
        w_uv_bd=_pack_wuv_blockdiag(w_uv[l]), w_out=w_out[l], w_out_b=b16(w_out[l]),
        g1=ln1_g[l][None, :], b1=ln1_b[l][None, :], w_xq=w_xq[l], w_xq_b=b16(w_xq[l]),
        w_mk_b=b16(w_mk[l]), w_mv_b=b16(w_mv[l]), w_xo=w_xo[l], w_xo_b=b16(w_xo[l]),
        g2=ln2_g[l][None, :], b2=ln2_b[l][None, :], w_rt=w_rt, b_rt=b_rt,
        w_eg_b=b16(w_e_gate[l]), w_eu_b=b16(w_e_up[l]), w_ed_b=b16(w_e_down[l]),
        g3=ln3_g[l][None, :], b3=ln3_b[l][None, :],
    )
    lp = x_prompt.shape[1]
    past_len = page_table.shape[1] * cache_latent.shape[2]
    y_p, s_p, lat_p, kpe_p, mk_p, mv_p = _prompt_group(
        x_prompt, mem_prompt, wp, _rope_tables(jnp.arange(lp)))
    y_d, s_d, lat_d, kpe_d = _sample_group(
        x_sample, state_gla[l], cache_latent, cache_krope, cache_mem_k, cache_mem_v, page_table, wp,
        _rope_tables(past_len + jnp.arange(1)))
    return (y_p, y_d, s_p, lat_p, kpe_p, mk_p, mv_p, s_d, lat_d, kpe_d)
```
